```python
import jax, jax.numpy as jnp
from jax import lax
import numpy as np

D_MODEL = 2048
BATCH = 4
SEQ = 2048
DEPTH = 4

GRID_W = 64
CTX_LEN = 256
N_MIXERS = 3
NORM_EPS = 1e-6
RET_HEADS = 8
RET_DK = D_MODEL // RET_HEADS
RET_DV = 2 * RET_DK
RET_QK = RET_HEADS * RET_DK
RET_V = RET_HEADS * RET_DV
RET_CHUNK = 128
ROPE_BASE = 10000.0
GM_WIDTH = 2 * D_MODEL
GM_GROUPS = 8
GM_CHUNK = 128
CV_WIDTH = 2 * D_MODEL
CONV_K = 3

kernel_name = "hybrid_retention_gmlp_shortconv_dit"


def _rmsnorm(x, g):
    xf = x.astype(jnp.float32)
    y = xf * lax.rsqrt(jnp.mean(xf * xf, axis=-1, keepdims=True) + NORM_EPS)
    return y.astype(x.dtype) * g


def _layernorm(x):
    xf = x.astype(jnp.float32)
    mu = jnp.mean(xf, axis=-1, keepdims=True)
    var = jnp.mean(jnp.square(xf - mu), axis=-1, keepdims=True)
    return ((xf - mu) * lax.rsqrt(var + NORM_EPS)).astype(x.dtype)


def _split_heads(a, n_heads):
    b, t, _ = a.shape
    return a.reshape(b, t, n_heads, -1).transpose(0, 2, 1, 3)


def _merge_heads(a):
    b, h, t, d = a.shape
    return a.transpose(0, 2, 1, 3).reshape(b, t, h * d)


def _rope1d(x, pos):
    half = x.shape[-1] // 2
    freqs = ROPE_BASE ** (-jnp.arange(half, dtype=jnp.float32) / half)
    ang = pos.astype(jnp.float32)[:, None] * freqs[None, :]
    cos = jnp.cos(ang).astype(x.dtype)
    sin = jnp.sin(ang).astype(x.dtype)
    x1, x2 = x[..., :half], x[..., half:]
    return jnp.concatenate([x1 * cos - x2 * sin, x1 * sin + x2 * cos], axis=-1)


def _axial_rope(x, rows, cols):
    half = x.shape[-1] // 2
    return jnp.concatenate([_rope1d(x[..., :half], rows), _rope1d(x[..., half:], cols)], axis=-1)


def _maybe_flip(a, flip):
    return jnp.flip(a, axis=2) if flip else a


def _ret_chunk_scan(q, k, v, log_g, s0):
    b, h, t, _ = q.shape
    n = t // RET_CHUNK
    lg = log_g.astype(jnp.float32)
    idx = jnp.arange(RET_CHUNK, dtype=jnp.float32)
    diff = idx[:, None] - idx[None, :]
    intra = jnp.where(diff >= 0, jnp.exp(lg[:, None, None] * jnp.maximum(diff, 0.0)), 0.0).astype(q.dtype)
    q_dec = jnp.exp(lg[:, None] * (idx + 1.0)).astype(q.dtype)
    k_dec = jnp.exp(lg[:, None] * (RET_CHUNK - 1.0 - idx)).astype(q.dtype)
    chunk_dec = jnp.exp(lg * RET_CHUNK).astype(q.dtype)

    def split(a):
        return jnp.moveaxis(a.reshape(b, h, n, RET_CHUNK, a.shape[-1]), 2, 0)

    def step(s, inp):
        qc, kc, vc = inp
        scores = jnp.einsum('bhid,bhjd->bhij', qc, kc) * intra[None]
        o = (jnp.einsum('bhij,bhje->bhie', scores, vc)
             + jnp.einsum('bhid,bhde->bhie', qc, s) * q_dec[None, :, :, None])
        s = (s * chunk_dec[None, :, None, None]
             + jnp.einsum('bhjd,bhje->bhde', kc * k_dec[None, :, :, None], vc))
        return s, o

    s_fin, o = lax.scan(step, s0, (split(q), split(k), split(v)))
    o = jnp.moveaxis(o, 0, 2).reshape(b, h, t, v.shape[-1])
    return o, s_fin


def _ret_final_state(k, v, log_g):
    t = k.shape[2]
    w = jnp.exp(log_g.astype(jnp.float32)[:, None]
                * (t - 1.0 - jnp.arange(t, dtype=jnp.float32))[None, :]).astype(k.dtype)
    return jnp.einsum('bhtd,bhte->bhde', k * w[None, :, :, None], v)


def _retention_mixer(h_ctx, h_lat, rows, cols, w_in, w_out, decay_logit, need_ctx):
    scale = RET_DK ** -0.5
    q_l, k_l, v_l, z_l = jnp.split(h_lat @ w_in, [RET_QK, 2 * RET_QK, 2 * RET_QK + RET_V], axis=-1)
    q_l = _axial_rope(_split_heads(q_l, RET_HEADS), rows, cols)
    k_l = _axial_rope(_split_heads(k_l, RET_HEADS) * scale, rows, cols)
    v_l = _split_heads(v_l, RET_HEADS)
    if need_ctx:
        q_c, k_c, v_c, z_c = jnp.split(h_ctx @ w_in, [RET_QK, 2 * RET_QK, 2 * RET_QK + RET_V], axis=-1)
        q_c = _split_heads(q_c, RET_HEADS)
    else:
        k_c, v_c = jnp.split(h_ctx @ w_in[:, RET_QK:2 * RET_QK + RET_V], [RET_QK], axis=-1)
    k_c = _split_heads(k_c, RET_HEADS) * scale
    v_c = _split_heads(v_c, RET_HEADS)

    log_g = jax.nn.log_sigmoid(decay_logit.astype(jnp.float32))
    o_lat_dirs = []
    o_ctx_dirs = []
    for d in range(2):
        flip = d == 1
        if need_ctx:
            s0 = jnp.zeros((k_c.shape[0], RET_HEADS, RET_DK, RET_DV), k_c.dtype)
            oc, s_ctx = _ret_chunk_scan(_maybe_flip(q_c, flip), _maybe_flip(k_c, flip),
                                        _maybe_flip(v_c, flip), log_g[d], s0)
            o_ctx_dirs.append(_maybe_flip(oc, flip))
        else:
            s_ctx = _ret_final_state(_maybe_flip(k_c, flip), _maybe_flip(v_c, flip), log_g[d])
        ol, _ = _ret_chunk_scan(_maybe_flip(q_l, flip), _maybe_flip(k_l, flip),
                                _maybe_flip(v_l, flip), log_g[d], s_ctx)
        o_lat_dirs.append(_maybe_flip(ol, flip))

    def finish(o, z):
        o = _merge_heads(_layernorm(o))
        return (jax.nn.silu(z) * o) @ w_out

    y_lat = finish(o_lat_dirs[0] + o_lat_dirs[1], z_l)
    y_ctx = finish(o_ctx_dirs[0] + o_ctx_dirs[1], z_c) if need_ctx else None
    return y_ctx, y_lat


def _spatial_gate(u, v, v_g, w_s, b_s):
    b, t, e = v.shape
    v = _layernorm(v) * v_g
    vb = v.reshape(b, t // GM_CHUNK, GM_CHUNK, GM_GROUPS, e // GM_GROUPS)
    s = jnp.einsum('gij,bnjgc->bnigc', w_s, vb) + b_s.T[None, None, :, :, None]
    return u * s.reshape(b, t, e)


def _gmlp_mixer(h_ctx, h_lat, w_in, v_g, w_s, b_s, w_out, need_ctx):
    def branch(h):
        u, v, z = jnp.split(h @ w_in, 3, axis=-1)
        y = _spatial_gate(jax.nn.gelu(u), jax.nn.gelu(v), v_g, w_s, b_s)
        return (jax.nn.silu(z) * y) @ w_out
    return (branch(h_ctx) if need_ctx else None), branch(h_lat)


def _dwconv3(a, w, bias):
    y = lax.conv_general_dilated(a, w[:, None, :], window_strides=(1,), padding=[(1, 1)],
                                 dimension_numbers=('NWC', 'WIO', 'NWC'),
                                 feature_group_count=a.shape[-1])
    return y + bias


def _shortconv_mixer(h_ctx, h_lat, w_in, conv_w, conv_b, w_out, need_ctx):
    def branch(h):
        bg, cg, hh, z = jnp.split(h @ w_in, 4, axis=-1)
        y = bg * _dwconv3(cg * hh, conv_w, conv_b)
        return (jax.nn.silu(z) * y) @ w_out
    return (branch(h_ctx) if need_ctx else None), branch(h_lat)


def setup_inputs(seed: int = 0) -> dict:
    key = jax.random.key(seed)
    ks = jax.random.split(key, 20)
    f32 = jnp.float32
    n_a = len(range(0, DEPTH, N_MIXERS))
    n_b = len(range(1, DEPTH, N_MIXERS))
    n_c = len(range(2, DEPTH, N_MIXERS))

    def nrm(k, shape, s):
        return jax.random.normal(k, shape, f32) * s

    gamma0 = 1.0 - 2.0 ** (-5.0 - jnp.arange(RET_HEADS, dtype=f32))
    return {
        "x": nrm(ks[0], (BATCH, SEQ, D_MODEL), 1.0),
        "c": nrm(ks[1], (BATCH, D_MODEL), 1.0),
        "ctx": nrm(ks[2], (BATCH, CTX_LEN, D_MODEL), 1.0),
        "c_ctx": nrm(ks[3], (D_MODEL,), 1.0),
        "norm_g": 1.0 + nrm(ks[4], (DEPTH, D_MODEL), 0.02),
        "ada_w": nrm(ks[5], (DEPTH, D_MODEL, 3 * D_MODEL), 0.5 * D_MODEL ** -0.5),
        "ada_b": nrm(ks[6], (DEPTH, 3 * D_MODEL), 0.02),
        "final_g": 1.0 + nrm(ks[7], (D_MODEL,), 0.02),
        "ret_w_in": nrm(ks[8], (n_a, D_MODEL, 2 * RET_QK + 2 * RET_V), D_MODEL ** -0.5),
        "ret_w_out": nrm(ks[9], (n_a, RET_V, D_MODEL), RET_V ** -0.5),
        "ret_decay": jnp.log(gamma0 / (1.0 - gamma0)) + nrm(ks[10], (n_a, 2, RET_HEADS), 0.1),
        "gm_w_in": nrm(ks[11], (n_b, D_MODEL, 3 * GM_WIDTH), D_MODEL ** -0.5),
        "gm_v_g": 1.0 + nrm(ks[12], (n_b, GM_WIDTH), 0.02),
        "gm_w_s": nrm(ks[13], (n_b, GM_GROUPS, GM_CHUNK, GM_CHUNK), GM_CHUNK ** -0.5),
        "gm_b_s": 1.0 + nrm(ks[14], (n_b, GM_GROUPS, GM_CHUNK), 0.02),
        "gm_w_out": nrm(ks[15], (n_b, GM_WIDTH, D_MODEL), GM_WIDTH ** -0.5),
        "cv_w_in": nrm(ks[16], (n_c, D_MODEL, 4 * CV_WIDTH), D_MODEL ** -0.5),
        "cv_conv_w": nrm(ks[17], (n_c, CONV_K, CV_WIDTH), CONV_K ** -0.5),
        "cv_conv_b": nrm(ks[18], (n_c, CV_WIDTH), 0.01),
        "cv_w_out": nrm(ks[19], (n_c, CV_WIDTH, D_MODEL), CV_WIDTH ** -0.5),
    }


def reference(x, c, ctx, c_ctx, norm_g, ada_w, ada_b, final_g,
              ret_w_in, ret_w_out, ret_decay,
              gm_w_in, gm_v_g, gm_w_s, gm_b_s, gm_w_out,
              cv_w_in, cv_conv_w, cv_conv_b, cv_w_out):
    n_lat = x.shape[1]
    ROWS = n_lat // GRID_W
    rows = jnp.repeat(jnp.arange(ROWS), GRID_W)
    cols = jnp.tile(jnp.arange(GRID_W), ROWS)
    sc = jax.nn.silu(c)
    sc_ctx = jax.nn.silu(c_ctx)
    ia = ib = ic = 0
    for i in range(DEPTH):
        need_ctx = i < DEPTH - 1
        shift, scale, gate = jnp.split(sc @ ada_w[i] + ada_b[i], 3, axis=-1)
        shift_c, scale_c, gate_c = jnp.split(sc_ctx @ ada_w[i] + ada_b[i], 3, axis=-1)
        h_lat = _rmsnorm(x, norm_g[i]) * (1.0 + scale[:, None]) + shift[:, None]
        h_ctx = _rmsnorm(ctx, norm_g[i]) * (1.0 + scale_c) + shift_c
        kind = i % N_MIXERS
        if kind == 0:
            y_ctx, y_lat = _retention_mixer(h_ctx, h_lat, rows, cols, ret_w_in[ia], ret_w_out[ia],
                                            ret_decay[ia], need_ctx)
            ia += 1
        elif kind == 1:
            y_ctx, y_lat = _gmlp_mixer(h_ctx, h_lat, gm_w_in[ib], gm_v_g[ib], gm_w_s[ib], gm_b_s[ib],
                                       gm_w_out[ib], need_ctx)
            ib += 1
        else:
            y_ctx, y_lat = _shortconv_mixer(h_ctx, h_lat, cv_w_in[ic], cv_conv_w[ic], cv_conv_b[ic],
                                            cv_w_out[ic], need_ctx)
            ic += 1
        x = x + gate[:, None] * y_lat
        if need_ctx:
            ctx = ctx + gate_c * y_ctx
    return _rmsnorm(x, final_g)
```

```python
import functools

import jax
import jax.numpy as jnp
import numpy as np
from jax import lax
from jax.experimental import pallas as pl
from jax.experimental.pallas import tpu as pltpu

D_MODEL = 2048
DEPTH = 4
GRID_W = 64
N_MIXERS = 3
NORM_EPS = 1e-6
RET_HEADS = 8
RET_DK = D_MODEL // RET_HEADS
RET_DV = 2 * RET_DK
RET_QK = RET_HEADS * RET_DK
RET_V = RET_HEADS * RET_DV
CHUNK = 128
ROPE_BASE = 10000.0
GM_WIDTH = 2 * D_MODEL
GM_GROUPS = 8
CV_WIDTH = 2 * D_MODEL
MIX_WIDTH = 2 * D_MODEL

V7X_LANES = 128
V7X_VMEM_LIMIT_BYTES = 56 * 1024 * 1024

MOD_ROWS = 8

BF16 = jnp.bfloat16
F32 = jnp.float32


def _params(*sem):
    return pltpu.CompilerParams(dimension_semantics=sem, vmem_limit_bytes=V7X_VMEM_LIMIT_BYTES)


def _silu(x):
    return x / (1.0 + jnp.exp(-x))


def _ada_kernel(c_ref, w_ref, b_ref, o_ref):
    sc = _silu(c_ref[...]).astype(BF16)
    o_ref[...] = jnp.dot(sc, w_ref[...].astype(BF16), preferred_element_type=F32) + b_ref[...]


def _ada(cond, ada_w, ada_b):
    depth, d, n = ada_w.shape
    tn = 1024
    return pl.pallas_call(
        _ada_kernel,
        grid=(depth, n // tn),
        in_specs=[
            pl.BlockSpec((MOD_ROWS, d), lambda l, j: (0, 0)),
            pl.BlockSpec((None, d, tn), lambda l, j: (l, 0, j)),
            pl.BlockSpec((None, 1, tn), lambda l, j: (l, 0, j)),
        ],
        out_specs=pl.BlockSpec((None, MOD_ROWS, tn), lambda l, j: (l, 0, j)),
        out_shape=jax.ShapeDtypeStruct((depth, MOD_ROWS, n), F32),
        compiler_params=_params("arbitrary", "arbitrary"),
        name="ada_mod",
    )(cond, ada_w, ada_b.reshape(depth, 1, n))


def _norm_mod_kernel(n_batch, x_ref, g_ref, mod_ref, h_ref):
    b = pl.program_id(0)
    t = pl.program_id(1)
    row = jnp.where(t == 0, n_batch, b)
    m = mod_ref[pl.ds(row, 1), :]
    shift = m[:, :D_MODEL]
    scale = m[:, D_MODEL:2 * D_MODEL]
    x = x_ref[...]
    y = x * lax.rsqrt(jnp.mean(x * x, axis=-1, keepdims=True) + NORM_EPS)
    h_ref[...] = ((y * g_ref[...]) * (1.0 + scale) + shift).astype(BF16)


def _norm_mod(x_all, g, mod, n_batch, ctx_len):
    m, d = x_all.shape
    per_batch = m // n_batch
    tiles = per_batch // ctx_len
    return pl.pallas_call(
        functools.partial(_norm_mod_kernel, n_batch),
        grid=(n_batch, tiles),
        in_specs=[
            pl.BlockSpec((ctx_len, d), lambda b, t: (b * tiles + t, 0)),
            pl.BlockSpec((1, d), lambda b, t: (0, 0)),
            pl.BlockSpec((MOD_ROWS, 3 * d), lambda b, t: (0, 0)),
        ],
        out_specs=pl.BlockSpec((ctx_len, d), lambda b, t: (b * tiles + t, 0)),
        out_shape=jax.ShapeDtypeStruct((m, d), BF16),
        compiler_params=_params("arbitrary", "arbitrary"),
        name="norm_mod",
    )(x_all, g.reshape(1, d), mod)


def _mm_in_kernel(h_ref, w_ref, o_ref, wb_ref):
    @pl.when(pl.program_id(1) == 0)
    def _():
        wb_ref[...] = w_ref[...].astype(BF16)

    o_ref[...] = jnp.dot(h_ref[...], wb_ref[...], preferred_element_type=F32).astype(BF16)


def _mm_in(h, w_stack, layer):
    m, k = h.shape
    n = w_stack.shape[-1]
    tm, tn = 1024, 1024
    return pl.pallas_call(
        _mm_in_kernel,
        grid=(n // tn, m // tm),
        in_specs=[
            pl.BlockSpec((tm, k), lambda j, i: (i, 0)),
            pl.BlockSpec((None, k, tn), lambda j, i: (layer, 0, j)),
        ],
        out_specs=pl.BlockSpec((tm, tn), lambda j, i: (i, j)),
        out_shape=jax.ShapeDtypeStruct((m, n), BF16),
        scratch_shapes=[pltpu.VMEM((k, tn), BF16)],
        compiler_params=_params("arbitrary", "arbitrary"),
        name="mm_in",
    )(h, w_stack)


def _mm_out_kernel(n_batch, ctx_len, tiles_per_batch, u_ref, w_ref, x_ref, mod_ref, o_ref, wb_ref):
    i = pl.program_id(1)

    @pl.when(i == 0)
    def _():
        wb_ref[...] = w_ref[...].astype(BF16)

    y = jnp.dot(u_ref[...], wb_ref[...], preferred_element_type=F32)
    b = i // tiles_per_batch
    gate_b = mod_ref[pl.ds(b, 1), :]
    gate_c = mod_ref[pl.ds(n_batch, 1), :]
    tm = y.shape[0]
    row_in_batch = (i % tiles_per_batch) * tm + lax.broadcasted_iota(jnp.int32, (tm, 1), 0)
    gate = jnp.where(row_in_batch < ctx_len, gate_c, gate_b)
    o_ref[...] = x_ref[...] + gate * y


def _mm_out(u, w_stack, layer, x_all, mod, n_batch, ctx_len):
    m, k = u.shape
    d = w_stack.shape[-1]
    per_batch = m // n_batch
    tm, tn = per_batch // 3, 512
    tiles_per_batch = per_batch // tm
    gate_blk = 2 * d // tn
    return pl.pallas_call(
        functools.partial(_mm_out_kernel, n_batch, ctx_len, tiles_per_batch),
        grid=(d // tn, m // tm),
        in_specs=[
            pl.BlockSpec((tm, k), lambda j, i: (i, 0)),
            pl.BlockSpec((None, k, tn), lambda j, i: (layer, 0, j)),
            pl.BlockSpec((tm, tn), lambda j, i: (i, j)),
            pl.BlockSpec((MOD_ROWS, tn), lambda j, i: (0, gate_blk + j)),
        ],
        out_specs=pl.BlockSpec((tm, tn), lambda j, i: (i, j)),
        out_shape=jax.ShapeDtypeStruct((m, d), F32),
        scratch_shapes=[pltpu.VMEM((k, tn), BF16)],
        input_output_aliases={2: 0},
        compiler_params=_params("arbitrary", "arbitrary"),
        name="mm_out",
    )(u, w_stack, x_all, mod)


def _swap_halves(x):
    half = V7X_LANES // 2
    parts = [pltpu.roll(x[:, s:s + V7X_LANES], half, 1) for s in range(0, x.shape[1], V7X_LANES)]
    return jnp.concatenate(parts, axis=1)


def _retention_kernel(ctx_len, need_ctx, lg_ref, q_ref, k_ref, v_ref, z_ref, cos_ref, sin_ref,
                      u_ref, qr_ref, kr_ref, oacc_ref, sf_ref, sb_ref):
    h = pl.program_id(1)
    n_rows = q_ref.shape[0]
    n_lat_chunks = (n_rows - ctx_len) // CHUNK
    n_ctx_chunks = ctx_len // CHUNK
    scale = RET_DK ** -0.5

    qr_ref[0:ctx_len, :] = q_ref[0:ctx_len, :]
    kr_ref[0:ctx_len, :] = k_ref[0:ctx_len, :]

    def rope_body(c, carry):
        rows = pl.ds(pl.multiple_of(ctx_len + c * CHUNK, CHUNK), CHUNK)
        trow = pl.ds(pl.multiple_of(c * CHUNK, CHUNK), CHUNK)
        cos = cos_ref[trow, :]
        sin = sin_ref[trow, :]
        for src, dst in ((q_ref, qr_ref), (k_ref, kr_ref)):
            x = src[rows, :].astype(F32)
            dst[rows, :] = (x * cos + _swap_halves(x) * sin).astype(BF16)
        return carry

    lax.fori_loop(0, n_lat_chunks, rope_body, 0)

    ii = lax.broadcasted_iota(jnp.int32, (CHUNK, CHUNK), 0)
    jj = lax.broadcasted_iota(jnp.int32, (CHUNK, CHUNK), 1)
    col = lax.broadcasted_iota(jnp.int32, (CHUNK, 1), 0).astype(F32)
    lg_f = lg_ref[0, h]
    lg_b = lg_ref[1, h]
    dfw = (ii - jj).astype(F32)
    intra_f = jnp.where(dfw >= 0, jnp.exp(lg_f * jnp.maximum(dfw, 0.0)), 0.0) * scale
    intra_b = jnp.where(dfw <= 0, jnp.exp(lg_b * jnp.maximum(-dfw, 0.0)), 0.0) * scale
    qdec_f = jnp.exp(lg_f * (col + 1.0))
    qdec_b = jnp.exp(lg_b * (CHUNK - col))
    kdec_f = jnp.exp(lg_f * (CHUNK - 1.0 - col)) * scale
    kdec_b = jnp.exp(lg_b * col) * scale
    cdec_f = jnp.exp(jnp.full((1, 1), lg_f * CHUNK, F32))
    cdec_b = jnp.exp(jnp.full((1, 1), lg_b * CHUNK, F32))

    sf_ref[...] = jnp.zeros_like(sf_ref)
    sb_ref[...] = jnp.zeros_like(sb_ref)

    def one_direction(row0, s_ref, intra, qdec, kdec, cdec, want_out):
        rows = pl.ds(pl.multiple_of(row0, CHUNK), CHUNK)
        kc = kr_ref[rows, :]
        vc = v_ref[rows, :]
        o = None
        if want_out:
            qc = qr_ref[rows, :]
            s = lax.dot_general(qc, kc, (((1,), (1,)), ((), ())), preferred_element_type=F32)
            s = (s * intra).astype(BF16)
            o = (jnp.dot(s, vc, preferred_element_type=F32)
                 + jnp.dot(qc, s_ref[...].astype(BF16), preferred_element_type=F32) * qdec)
        kd = (kc.astype(F32) * kdec).astype(BF16)
        s_ref[...] = s_ref[...] * cdec + lax.dot_general(
            kd, vc, (((0,), (0,)), ((), ())), preferred_element_type=F32)
        return o, rows

    def finish(o, rows):
        mu = jnp.mean(o, axis=-1, keepdims=True)
        var = jnp.mean(jnp.square(o - mu), axis=-1, keepdims=True)
        on = (o - mu) * lax.rsqrt(var + NORM_EPS)
        u_ref[rows, :] = (_silu(z_ref[rows, :].astype(F32)) * on).astype(BF16)

    def step(row_f, row_b, want_out, first_touch):
        of, rows_f = one_direction(row_f, sf_ref, intra_f, qdec_f, kdec_f, cdec_f, want_out)
        ob, rows_b = one_direction(row_b, sb_ref, intra_b, qdec_b, kdec_b, cdec_b, want_out)
        if not want_out:
            return
        if first_touch:
            oacc_ref[rows_f, :] = of
            oacc_ref[rows_b, :] = ob
        else:
            finish(oacc_ref[rows_f, :] + of, rows_f)
            finish(oacc_ref[rows_b, :] + ob, rows_b)

    for i in range(n_ctx_chunks):
        step(i * CHUNK, (n_ctx_chunks - 1 - i) * CHUNK, need_ctx, i < n_ctx_chunks // 2)
    if not need_ctx:
        u_ref[0:ctx_len, :] = jnp.zeros((ctx_len, u_ref.shape[1]), BF16)

    def lat_body(first_touch, j, carry):
        step(ctx_len + j * CHUNK, ctx_len + (n_lat_chunks - 1 - j) * CHUNK, True, first_touch)
        return carry

    lax.fori_loop(0, n_lat_chunks // 2, functools.partial(lat_body, True), 0)
    lax.fori_loop(n_lat_chunks // 2, n_lat_chunks, functools.partial(lat_body, False), 0)


def _rope_tables(seq):
    half = RET_DK // 4
    t = np.arange(seq)
    freqs = (ROPE_BASE ** (-jnp.arange(half, dtype=F32) / half))
    ang_r = jnp.asarray(t // GRID_W, F32)[:, None] * freqs[None, :]
    ang_c = jnp.asarray(t % GRID_W, F32)[:, None] * freqs[None, :]
    cos = jnp.concatenate([jnp.cos(ang_r)] * 2 + [jnp.cos(ang_c)] * 2, axis=1)
    sin = jnp.concatenate([-jnp.sin(ang_r), jnp.sin(ang_r), -jnp.sin(ang_c), jnp.sin(ang_c)], axis=1)
    return cos, sin


def _retention(p, log_g, n_batch, ctx_len, need_ctx):
    m = p.shape[0]
    per_batch = m // n_batch
    seq = per_batch - ctx_len
    cos, sin = _rope_tables(seq)
    qb, vb = RET_QK // RET_DK, (2 * RET_QK) // RET_DV
    return pl.pallas_call(
        functools.partial(_retention_kernel, ctx_len, need_ctx),
        grid=(n_batch, RET_HEADS),
        in_specs=[
            pl.BlockSpec(memory_space=pltpu.SMEM),
            pl.BlockSpec((per_batch, RET_DK), lambda b, h: (b, h)),
            pl.BlockSpec((per_batch, RET_DK), lambda b, h: (b, qb + h)),
            pl.BlockSpec((per_batch, RET_DV), lambda b, h: (b, vb + h)),
            pl.BlockSpec((per_batch, RET_DV), lambda b, h: (b, vb + RET_HEADS + h)),
            pl.BlockSpec((seq, RET_DK), lambda b, h: (0, 0)),
            pl.BlockSpec((seq, RET_DK), lambda b, h: (0, 0)),
        ],
        out_specs=pl.BlockSpec((per_batch, RET_DV), lambda b, h: (b, h)),
        out_shape=jax.ShapeDtypeStruct((m, RET_V), BF16),
        scratch_shapes=[
            pltpu.VMEM((per_batch, RET_DK), BF16),
            pltpu.VMEM((per_batch, RET_DK), BF16),
            pltpu.VMEM((per_batch, RET_DV), F32),
            pltpu.VMEM((RET_DK, RET_DV), F32),
            pltpu.VMEM((RET_DK, RET_DV), F32),
        ],
        compiler_params=_params("arbitrary", "arbitrary"),
        name="retention",
    )(log_g, p, p, p, p, cos, sin)


def _gmlp_kernel(u_ref, v_ref, z_ref, vg_ref, ws_ref, bs_ref, o_ref, gv_ref, vn_ref):
    rows = u_ref.shape[0]
    gw = GM_WIDTH // GM_GROUPS
    s1 = jnp.zeros((rows, 1), F32)
    for g in range(GM_GROUPS):
        cs = slice(g * gw, (g + 1) * gw)
        gv = jax.nn.gelu(v_ref[:, cs].astype(F32))
        gv_ref[:, cs] = gv
        s1 = s1 + jnp.sum(gv, axis=-1, keepdims=True)
    mu = s1 / GM_WIDTH
    s2 = jnp.zeros((rows, 1), F32)
    for g in range(GM_GROUPS):
        cs = slice(g * gw, (g + 1) * gw)
        s2 = s2 + jnp.sum(jnp.square(gv_ref[:, cs] - mu), axis=-1, keepdims=True)
    rstd = lax.rsqrt(s2 / GM_WIDTH + NORM_EPS)
    for g in range(GM_GROUPS):
        cs = slice(g * gw, (g + 1) * gw)
        vn_ref[:, cs] = (((gv_ref[:, cs] - mu) * rstd) * vg_ref[:, cs]).astype(BF16)
    for g in range(GM_GROUPS):
        cs = slice(g * gw, (g + 1) * gw)
        w = ws_ref[g].astype(BF16)
        bias = bs_ref[:, g:g + 1]
        for c in range(rows // CHUNK):
            rs = slice(c * CHUNK, (c + 1) * CHUNK)
            s = jnp.dot(w, vn_ref[rs, cs], preferred_element_type=F32) + bias
            y = jax.nn.gelu(u_ref[rs, cs].astype(F32)) * s
            o_ref[rs, cs] = (_silu(z_ref[rs, cs].astype(F32)) * y).astype(BF16)


def _gmlp(p, v_g, w_s, b_s):
    m = p.shape[0]
    tr = 2 * CHUNK
    return pl.pallas_call(
        _gmlp_kernel,
        grid=(m // tr,),
        in_specs=[
            pl.BlockSpec((tr, GM_WIDTH), lambda i: (i, 0)),
            pl.BlockSpec((tr, GM_WIDTH), lambda i: (i, 1)),
            pl.BlockSpec((tr, GM_WIDTH), lambda i: (i, 2)),
            pl.BlockSpec((1, GM_WIDTH), lambda i: (0, 0)),
            pl.BlockSpec((GM_GROUPS, CHUNK, CHUNK), lambda i: (0, 0, 0)),
            pl.BlockSpec((CHUNK, GM_GROUPS), lambda i: (0, 0)),
        ],
        out_specs=pl.BlockSpec((tr, GM_WIDTH), lambda i: (i, 0)),
        out_shape=jax.ShapeDtypeStruct((m, GM_WIDTH), BF16),
        scratch_shapes=[pltpu.VMEM((tr, GM_WIDTH), F32), pltpu.VMEM((tr, GM_WIDTH), BF16)],
        compiler_params=_params("arbitrary"),
        name="gmlp_mix",
    )(p, p, p, v_g.reshape(1, GM_WIDTH), w_s, b_s.T)


def _conv_kernel(ctx_len, bg_ref, cg_ref, hh_ref, z_ref, w_ref, b_ref, o_ref):
    n_rows = bg_ref.shape[0]
    p = cg_ref[...].astype(F32) * hh_ref[...].astype(F32)
    r = lax.broadcasted_iota(jnp.int32, (n_rows, 1), 0)
    prev_ok = (r != 0) & (r != ctx_len)
    next_ok = (r != ctx_len - 1) & (r != n_rows - 1)
    p_prev = jnp.where(prev_ok, pltpu.roll(p, 1, 0), 0.0)
    p_next = jnp.where(next_ok, pltpu.roll(p, n_rows - 1, 0), 0.0)
    w = w_ref[...]
    conv = p_prev * w[0:1, :] + p * w[1:2, :] + p_next * w[2:3, :] + b_ref[...]
    y = bg_ref[...].astype(F32) * conv
    o_ref[...] = (_silu(z_ref[...].astype(F32)) * y).astype(BF16)


def _shortconv(p, conv_w, conv_b, n_batch, ctx_len):
    m = p.shape[0]
    per_batch = m // n_batch
    tc = 256
    nb = CV_WIDTH // tc
    return pl.pallas_call(
        functools.partial(_conv_kernel, ctx_len),
        grid=(n_batch, nb),
        in_specs=[
            pl.BlockSpec((per_batch, tc), lambda b, j: (b, j)),
            pl.BlockSpec((per_batch, tc), lambda b, j: (b, nb + j)),
            pl.BlockSpec((per_batch, tc), lambda b, j: (b, 2 * nb + j)),
            pl.BlockSpec((per_batch, tc), lambda b, j: (b, 3 * nb + j)),
            pl.BlockSpec((conv_w.shape[0], tc), lambda b, j: (0, j)),
            pl.BlockSpec((1, tc), lambda b, j: (0, j)),
        ],
        out_specs=pl.BlockSpec((per_batch, tc), lambda b, j: (b, j)),
        out_shape=jax.ShapeDtypeStruct((m, CV_WIDTH), BF16),
        compiler_params=_params("arbitrary", "arbitrary"),
        name="shortconv_mix",
    )(p, p, p, p, conv_w, conv_b.reshape(1, CV_WIDTH))


def _final_norm_kernel(x_ref, g_ref, o_ref):
    x = x_ref[...]
    o_ref[...] = (x * lax.rsqrt(jnp.mean(x * x, axis=-1, keepdims=True) + NORM_EPS)) * g_ref[...]


def _final_norm(x_all, g, n_batch, ctx_len):
    m, d = x_all.shape
    per_batch = m // n_batch
    tiles = per_batch // ctx_len
    lat_tiles = tiles - 1
    return pl.pallas_call(
        _final_norm_kernel,
        grid=(n_batch, lat_tiles),
        in_specs=[
            pl.BlockSpec((ctx_len, d), lambda b, t: (b * tiles + 1 + t, 0)),
            pl.BlockSpec((1, d), lambda b, t: (0, 0)),
        ],
        out_specs=pl.BlockSpec((ctx_len, d), lambda b, t: (b * lat_tiles + t, 0)),
        out_shape=jax.ShapeDtypeStruct((n_batch * lat_tiles * ctx_len, d), F32),
        compiler_params=_params("arbitrary", "arbitrary"),
        name="final_norm",
    )(x_all, g.reshape(1, d))


def kernel(x, c, ctx, c_ctx, norm_g, ada_w, ada_b, final_g, ret_w_in, ret_w_out, ret_decay,
           gm_w_in, gm_v_g, gm_w_s, gm_b_s, gm_w_out, cv_w_in, cv_conv_w, cv_conv_b, cv_w_out):
    n_batch, seq, d = x.shape
    ctx_len = ctx.shape[1]
    assert d == D_MODEL and seq % CHUNK == 0 and ctx_len % CHUNK == 0 and seq % ctx_len == 0
    assert n_batch < MOD_ROWS

    x_all = jnp.concatenate([ctx, x], axis=1).reshape(n_batch * (ctx_len + seq), d)
    cond = jnp.zeros((MOD_ROWS, d), F32).at[:n_batch].set(c).at[n_batch].set(c_ctx)
    mods = _ada(cond, ada_w, ada_b)
    log_g = jax.nn.log_sigmoid(ret_decay.astype(F32))

    counts = [0] * N_MIXERS
    for layer in range(DEPTH):
        need_ctx = layer < DEPTH - 1
        kind = layer % N_MIXERS
        idx = counts[kind]
        counts[kind] += 1
        h = _norm_mod(x_all, norm_g[layer], mods[layer], n_batch, ctx_len)
        if kind == 0:
            p = _mm_in(h, ret_w_in, idx)
            u = _retention(p, log_g[idx], n_batch, ctx_len, need_ctx)
            w_out = ret_w_out
        elif kind == 1:
            p = _mm_in(h, gm_w_in, idx)
            u = _gmlp(p, gm_v_g[idx], gm_w_s[idx], gm_b_s[idx])
            w_out = gm_w_out
        else:
            p = _mm_in(h, cv_w_in, idx)
            u = _shortconv(p, cv_conv_w[idx], cv_conv_b[idx], n_batch, ctx_len)
            w_out = cv_w_out
        x_all = _mm_out(u, w_out, idx, x_all, mods[layer], n_batch, ctx_len)
    return _final_norm(x_all, final_g, n_batch, ctx_len).reshape(n_batch, seq, d)
```
